```python
import jax, jax.numpy as jnp
from jax import lax
import numpy as np


D_MODEL = 1024
BATCH = 2
SEQ = 8192
DEPTH = 1

D_MIX = D_MODEL
D_CONV = D_MIX // 2
D_GMLP = D_MIX - D_CONV
HEAD_DIM = 64
N_CONV_HEADS = D_CONV // HEAD_DIM
N_GMLP_HEADS = D_GMLP // HEAD_DIM
CONV_WIDTH = 31
CHUNK = 128
D_FF = 2816
FFN_CONV_WIDTH = 3
N_MOD = 6
RMS_EPS = 1e-6
LN_EPS = 1e-5

kernel_name = "hybrid_conv_gmlp_adaln_block"


def rms_norm(x, g):
    xf = x.astype(jnp.float32)
    y = xf * lax.rsqrt(jnp.mean(xf * xf, axis=-1, keepdims=True) + RMS_EPS)
    return (y * g.astype(jnp.float32)).astype(x.dtype)


def layer_norm(x, g, b):
    xf = x.astype(jnp.float32)
    mu = jnp.mean(xf, axis=-1, keepdims=True)
    var = jnp.mean(jnp.square(xf - mu), axis=-1, keepdims=True)
    y = (xf - mu) * lax.rsqrt(var + LN_EPS)
    return (y * g.astype(jnp.float32) + b.astype(jnp.float32)).astype(x.dtype)


def causal_dwconv(x, w, b):
    k = w.shape[0]
    ch = x.shape[-1]
    y = lax.conv_general_dilated(
        x, w[:, None, :].astype(x.dtype), window_strides=(1,), padding=[(k - 1, 0)],
        dimension_numbers=("NWC", "WIO", "NWC"), feature_group_count=ch)
    return y + b.astype(x.dtype)


def modulate(h, shift, scale):
    return h * (1.0 + scale[:, None, :]) + shift[:, None, :]


def setup_inputs(seed: int = 0) -> dict:
    key = jax.random.key(seed)
    ks = jax.random.split(key, 24)
    f32 = jnp.float32
    L, D = DEPTH, D_MODEL

    def nrm(k, shape, scale):
        return jax.random.normal(k, shape, f32) * scale

    return {
        "x": nrm(ks[0], (BATCH, SEQ, D), 1.0),
        "c": nrm(ks[1], (BATCH, D), 1.0),
        "w_ada": nrm(ks[2], (L, D, N_MOD * D), 0.5 * D ** -0.5),
        "b_ada": nrm(ks[3], (L, N_MOD * D), 0.01),
        "norm1_gain": 1.0 + nrm(ks[4], (L, D), 0.02),
        "w_in": nrm(ks[5], (L, D, 2 * D_MIX), D ** -0.5),
        "conv_dw_w": nrm(ks[6], (L, CONV_WIDTH, D_CONV), CONV_WIDTH ** -0.5),
        "conv_dw_b": nrm(ks[7], (L, D_CONV), 0.01),
        "conv_ln_g": 1.0 + nrm(ks[8], (L, D_CONV), 0.02),
        "conv_ln_b": nrm(ks[9], (L, D_CONV), 0.01),
        "gm_ln_g": 1.0 + nrm(ks[10], (L, D_GMLP), 0.02),
        "gm_ln_b": nrm(ks[11], (L, D_GMLP), 0.01),
        "gm_ws": nrm(ks[12], (L, N_GMLP_HEADS, CHUNK, CHUNK), CHUNK ** -0.5),
        "gm_bs": 1.0 + nrm(ks[13], (L, N_GMLP_HEADS, CHUNK), 0.01),
        "mix_out_gain": 1.0 + nrm(ks[14], (L, D_MIX), 0.02),
        "w_out": nrm(ks[15], (L, D_MIX, D), D_MIX ** -0.5),
        "norm2_gain": 1.0 + nrm(ks[16], (L, D), 0.02),
        "w_up": nrm(ks[17], (L, D, 2 * D_FF), D ** -0.5),
        "ffn_dw_w": nrm(ks[18], (L, FFN_CONV_WIDTH, 2 * D_FF), FFN_CONV_WIDTH ** -0.5),
        "ffn_dw_b": nrm(ks[19], (L, 2 * D_FF), 0.01),
        "w_down": nrm(ks[20], (L, D_FF, D), D_FF ** -0.5),
        "final_gain": 1.0 + nrm(ks[21], (D,), 0.02),
    }


def reference(x, c, w_ada, b_ada, norm1_gain, w_in, conv_dw_w, conv_dw_b, conv_ln_g, conv_ln_b,
              gm_ln_g, gm_ln_b, gm_ws, gm_bs, mix_out_gain, w_out, norm2_gain, w_up,
              ffn_dw_w, ffn_dw_b, w_down, final_gain):
    bsz, seq, _ = x.shape
    n_chunks = seq // CHUNK
    causal_mask = jnp.tril(jnp.ones((CHUNK, CHUNK), dtype=x.dtype))
    c_act = jax.nn.silu(c)

    for l in range(DEPTH):
        mod = c_act @ w_ada[l] + b_ada[l]
        sh1, sc1, gt1, sh2, sc2, gt2 = jnp.split(mod, N_MOD, axis=-1)

        h = modulate(rms_norm(x, norm1_gain[l]), sh1, sc1)
        z = h @ w_in[l]
        ca, cg, gu, gv = jnp.split(z, [D_CONV, 2 * D_CONV, 2 * D_CONV + D_GMLP], axis=-1)

        a = ca * jax.nn.sigmoid(cg)
        a = causal_dwconv(a, conv_dw_w[l], conv_dw_b[l])
        a = jax.nn.silu(layer_norm(a, conv_ln_g[l], conv_ln_b[l]))

        gu = jax.nn.gelu(gu)
        gv = layer_norm(jax.nn.gelu(gv), gm_ln_g[l], gm_ln_b[l])
        gv = gv.reshape(bsz, n_chunks, CHUNK, N_GMLP_HEADS, HEAD_DIM)
        ws = gm_ws[l] * causal_mask[None]
        sp = jnp.einsum("hts,bnshc->bnthc", ws, gv)
        sp = sp + jnp.transpose(gm_bs[l])[None, None, :, :, None]
        g = gu * sp.reshape(bsz, seq, D_GMLP)

        y = jnp.concatenate([rms_norm(a, mix_out_gain[l, :D_CONV]),
                             rms_norm(g, mix_out_gain[l, D_CONV:])], axis=-1)
        x = x + gt1[:, None, :] * (y @ w_out[l])

        h = modulate(rms_norm(x, norm2_gain[l]), sh2, sc2)
        up = causal_dwconv(h @ w_up[l], ffn_dw_w[l], ffn_dw_b[l])
        val, gate = jnp.split(up, 2, axis=-1)
        x = x + gt2[:, None, :] * ((jax.nn.silu(gate) * val) @ w_down[l])

    return rms_norm(x, final_gain)
```

```python
import functools

import jax
import jax.numpy as jnp
from jax import lax
from jax.experimental import pallas as pl
from jax.experimental.pallas import tpu as pltpu

D_MODEL = 1024
D_CONV = 512
D_GMLP = 512
HEAD_DIM = 64
N_GMLP_HEADS = 8
N_HEAD_PAIRS = N_GMLP_HEADS // 2
CONV_WIDTH = 31
CHUNK = 128
D_FF = 2816
FFN_CONV_WIDTH = 3
N_MOD = 6
RMS_EPS = 1e-6
LN_EPS = 1e-5

SUBLANES = 8
LANES = 128
CONV_HALO = 32
FFN_HALO = SUBLANES
ADA_TN = 512
MIX_T = 512
FFN_T = 512
FFN_FC = 256
VMEM_LIMIT_BYTES = 56 * 1024 * 1024

_GELU_C = 0.7978845608028654


def _sigmoid(v):
    return 1.0 / (1.0 + jnp.exp(-v))


def _gelu_tanh(v):
    return 0.5 * v * (1.0 + jnp.tanh(_GELU_C * (v + 0.044715 * (v * v * v))))


def _rms_scale(v, width):
    ms = jnp.sum(v * v, axis=-1, keepdims=True) * (1.0 / width)
    return v * lax.rsqrt(ms + RMS_EPS)


def _layer_norm(v, g, b, width):
    mu = jnp.sum(v, axis=-1, keepdims=True) * (1.0 / width)
    vc = v - mu
    var = jnp.sum(vc * vc, axis=-1, keepdims=True) * (1.0 / width)
    return vc * lax.rsqrt(var + LN_EPS) * g + b


def _ada_kernel(ct_ref, w_ref, b_ref, o_ref):
    ct = ct_ref[...]
    ca = ct * _sigmoid(ct)
    w = w_ref[...]
    rows = [jnp.sum(w * ca[:, b:b + 1], axis=0, keepdims=True) for b in range(ct.shape[1])]
    o_ref[...] = jnp.concatenate(rows, axis=0) + b_ref[...]


def _ada_call(c, w_ada, b_ada):
    bsz, d = c.shape
    n = w_ada.shape[1]
    return pl.pallas_call(
        _ada_kernel,
        grid=(n // ADA_TN,),
        in_specs=[
            pl.BlockSpec((d, bsz), lambda j: (0, 0)),
            pl.BlockSpec((d, ADA_TN), lambda j: (0, j)),
            pl.BlockSpec((1, ADA_TN), lambda j: (0, j)),
        ],
        out_specs=pl.BlockSpec((bsz, ADA_TN), lambda j: (0, j)),
        out_shape=jax.ShapeDtypeStruct((bsz, n), jnp.float32),
        compiler_params=pltpu.CompilerParams(dimension_semantics=("arbitrary",)),
        name="ada",
    )(c.T, w_ada, b_ada.reshape(1, n))


def _mixer_kernel(x_ref, mod_ref, g1_ref, win_ref, cw_ref, cb_ref, clg_ref, clb_ref,
                  glg_ref, glb_ref, ws_ref, bst_ref, og_ref, wout_ref,
                  o_ref, a_scr, wsp_scr):
    b = pl.program_id(0)
    s = pl.program_id(1)
    t = x_ref.shape[1]
    n_chunks = t // CHUNK

    @pl.when(jnp.logical_and(b == 0, s == 0))
    def _():
        row = lax.broadcasted_iota(jnp.int32, (CHUNK, CHUNK), 0)
        col = lax.broadcasted_iota(jnp.int32, (CHUNK, CHUNK), 1)
        keep = col <= row
        for h in range(N_GMLP_HEADS):
            w = jnp.where(keep, ws_ref[h], 0.0).astype(jnp.bfloat16)
            wsp_scr[h // 2, :, (h % 2) * CHUNK:(h % 2 + 1) * CHUNK] = w

    @pl.when(s == 0)
    def _():
        a_scr[0:CONV_HALO, :] = jnp.zeros((CONV_HALO, D_CONV), jnp.float32)

    x = x_ref[0]
    sh1 = mod_ref[0, :, 0:D_MODEL]
    sc1 = mod_ref[0, :, D_MODEL:2 * D_MODEL]
    gt1 = mod_ref[0, :, 2 * D_MODEL:3 * D_MODEL]
    h = _rms_scale(x, D_MODEL) * (g1_ref[...] * (1.0 + sc1)) + sh1
    hb = h.astype(jnp.bfloat16)

    def proj(lo, width):
        return jnp.dot(hb, win_ref[:, lo:lo + width], preferred_element_type=jnp.float32)

    a = proj(0, D_CONV) * _sigmoid(proj(D_CONV, D_CONV))
    a_scr[CONV_HALO:CONV_HALO + t, :] = a
    base = CONV_HALO - (CONV_WIDTH - 1)
    acc = cw_ref[0:1, :] * a_scr[pl.ds(base, t), :]
    for k in range(1, CONV_WIDTH):
        acc = acc + cw_ref[k:k + 1, :] * a_scr[pl.ds(base + k, t), :]
    acc = acc + cb_ref[...]
    a_scr[0:CONV_HALO, :] = a_scr[t:t + CONV_HALO, :]
    av = _layer_norm(acc, clg_ref[...], clb_ref[...], D_CONV)
    av = av * _sigmoid(av)
    ya = _rms_scale(av, D_CONV) * og_ref[:, 0:D_CONV]

    gu = _gelu_tanh(proj(2 * D_CONV, D_GMLP))
    gv = _layer_norm(_gelu_tanh(proj(2 * D_CONV + D_GMLP, D_GMLP)),
                     glg_ref[...], glb_ref[...], D_GMLP).astype(jnp.bfloat16)
    lane = lax.broadcasted_iota(jnp.int32, (CHUNK, 2 * HEAD_DIM), 1)
    low = lane < HEAD_DIM
    zero = jnp.zeros((CHUNK, 2 * HEAD_DIM), jnp.bfloat16)
    sp_rows = []
    for j in range(n_chunks):
        sp_cols = []
        for p in range(N_HEAD_PAIRS):
            gvp = gv[j * CHUNK:(j + 1) * CHUNK, p * 2 * HEAD_DIM:(p + 1) * 2 * HEAD_DIM]
            rhs = jnp.concatenate([jnp.where(low, gvp, zero), jnp.where(low, zero, gvp)], axis=0)
            sp_cols.append(jnp.dot(wsp_scr[p], rhs, preferred_element_type=jnp.float32))
        sp_rows.append(jnp.concatenate(sp_cols, axis=1) + bst_ref[...])
    g = gu * jnp.concatenate(sp_rows, axis=0)
    yg = _rms_scale(g, D_GMLP) * og_ref[:, D_CONV:D_CONV + D_GMLP]

    out = (jnp.dot(ya.astype(jnp.bfloat16), wout_ref[0:D_CONV, :], preferred_element_type=jnp.float32)
           + jnp.dot(yg.astype(jnp.bfloat16), wout_ref[D_CONV:D_CONV + D_GMLP, :],
                     preferred_element_type=jnp.float32))
    o_ref[0] = x + gt1 * out


def _const_spec(shape):
    zeros = (0,) * len(shape)
    return pl.BlockSpec(shape, lambda b, s: zeros, pipeline_mode=pl.Buffered(1))


def _mixer_call(x, mod3, g1, w_in, cw, cb, clg, clb, glg, glb, ws, bst, og, w_out):
    bsz, seq, d = x.shape
    t = MIX_T
    tile = pl.BlockSpec((1, t, d), lambda b, s: (b, s, 0))
    return pl.pallas_call(
        _mixer_kernel,
        grid=(bsz, seq // t),
        in_specs=[
            tile,
            pl.BlockSpec((1, 1, N_MOD * d), lambda b, s: (b, 0, 0)),
            _const_spec((1, d)),
            _const_spec((d, 2 * d)),
            _const_spec((CONV_WIDTH, D_CONV)),
            _const_spec((1, D_CONV)),
            _const_spec((1, D_CONV)),
            _const_spec((1, D_CONV)),
            _const_spec((1, D_GMLP)),
            _const_spec((1, D_GMLP)),
            _const_spec((N_GMLP_HEADS, CHUNK, CHUNK)),
            _const_spec((CHUNK, D_GMLP)),
            _const_spec((1, d)),
            _const_spec((d, d)),
        ],
        out_specs=tile,
        out_shape=jax.ShapeDtypeStruct(x.shape, jnp.float32),
        scratch_shapes=[
            pltpu.VMEM((CONV_HALO + t, D_CONV), jnp.float32),
            pltpu.VMEM((N_HEAD_PAIRS, CHUNK, 2 * CHUNK), jnp.bfloat16),
        ],
        compiler_params=pltpu.CompilerParams(
            dimension_semantics=("arbitrary", "arbitrary"),
            vmem_limit_bytes=VMEM_LIMIT_BYTES),
        name="mixer",
    )(x, mod3, g1, w_in, cw, cb, clg, clb, glg, glb, ws, bst, og, w_out)


def _ffn_kernel(x_ref, mod_ref, g2_ref, wup_ref, dw_ref, db_ref, wdn_ref, gf_ref,
                o_ref, u_scr, hist_scr, act_scr):
    s = pl.program_id(1)
    t = x_ref.shape[1]
    n_fc = D_FF // FFN_FC

    @pl.when(s == 0)
    def _():
        hist_scr[...] = jnp.zeros(hist_scr.shape, jnp.float32)

    x = x_ref[0]
    sh2 = mod_ref[0, :, 3 * D_MODEL:4 * D_MODEL]
    sc2 = mod_ref[0, :, 4 * D_MODEL:5 * D_MODEL]
    gt2 = mod_ref[0, :, 5 * D_MODEL:6 * D_MODEL]
    h = _rms_scale(x, D_MODEL) * (g2_ref[...] * (1.0 + sc2)) + sh2
    hb = h.astype(jnp.bfloat16)

    def conv_cols(slot, lo):
        u = jnp.dot(hb, wup_ref[:, lo:lo + FFN_FC], preferred_element_type=jnp.float32)
        u_scr[slot, 0:FFN_HALO, :] = hist_scr[:, lo:lo + FFN_FC]
        u_scr[slot, FFN_HALO:FFN_HALO + t, :] = u
        hist_scr[:, lo:lo + FFN_FC] = u[t - FFN_HALO:t, :]
        base = FFN_HALO - (FFN_CONV_WIDTH - 1)
        acc = dw_ref[0:1, lo:lo + FFN_FC] * u_scr[slot, pl.ds(base, t), :]
        for k in range(1, FFN_CONV_WIDTH):
            acc = acc + dw_ref[k:k + 1, lo:lo + FFN_FC] * u_scr[slot, pl.ds(base + k, t), :]
        return acc + db_ref[:, lo:lo + FFN_FC]

    for c in range(n_fc):
        val = conv_cols(0, c * FFN_FC)
        gate = conv_cols(1, D_FF + c * FFN_FC)
        act_scr[:, c * FFN_FC:(c + 1) * FFN_FC] = (gate * _sigmoid(gate) * val).astype(jnp.bfloat16)

    y = jnp.dot(act_scr[...], wdn_ref[...], preferred_element_type=jnp.float32)
    x2 = x + gt2 * y
    o_ref[0] = _rms_scale(x2, D_MODEL) * gf_ref[...]


def _ffn_call(x1, mod3, g2, w_up, dw, db, w_down, gf):
    bsz, seq, d = x1.shape
    t = FFN_T
    tile = pl.BlockSpec((1, t, d), lambda b, s: (b, s, 0))
    return pl.pallas_call(
        _ffn_kernel,
        grid=(bsz, seq // t),
        in_specs=[
            tile,
            pl.BlockSpec((1, 1, N_MOD * d), lambda b, s: (b, 0, 0)),
            _const_spec((1, d)),
            _const_spec((d, 2 * D_FF)),
            _const_spec((FFN_CONV_WIDTH, 2 * D_FF)),
            _const_spec((1, 2 * D_FF)),
            _const_spec((D_FF, d)),
            _const_spec((1, d)),
        ],
        out_specs=tile,
        out_shape=jax.ShapeDtypeStruct(x1.shape, jnp.float32),
        scratch_shapes=[
            pltpu.VMEM((2, FFN_HALO + t, FFN_FC), jnp.float32),
            pltpu.VMEM((FFN_HALO, 2 * D_FF), jnp.float32),
            pltpu.VMEM((t, D_FF), jnp.bfloat16),
        ],
        compiler_params=pltpu.CompilerParams(
            dimension_semantics=("arbitrary", "arbitrary"),
            vmem_limit_bytes=VMEM_LIMIT_BYTES),
        name="ffn",
    )(x1, mod3, g2, w_up, dw, db, w_down, gf)


def kernel(x, c, w_ada, b_ada, norm1_gain, w_in, conv_dw_w, conv_dw_b, conv_ln_g, conv_ln_b,
           gm_ln_g, gm_ln_b, gm_ws, gm_bs, mix_out_gain, w_out, norm2_gain, w_up,
           ffn_dw_w, ffn_dw_b, w_down, final_gain):
    bsz, seq, d = x.shape
    assert (d, seq % MIX_T, seq % FFN_T, MIX_T % CHUNK) == (D_MODEL, 0, 0, 0)
    assert w_ada.shape[0] == 1, "single-layer block"
    bf16 = jnp.bfloat16
    row = lambda v: v.reshape(1, -1)

    mod = _ada_call(c, w_ada[0], b_ada[0])
    mod3 = mod.reshape(bsz, 1, N_MOD * d)
    bst = jnp.repeat(gm_bs[0].T, HEAD_DIM, axis=1)

    x1 = _mixer_call(x, mod3, row(norm1_gain[0]), w_in[0].astype(bf16), conv_dw_w[0],
                     row(conv_dw_b[0]), row(conv_ln_g[0]), row(conv_ln_b[0]),
                     row(gm_ln_g[0]), row(gm_ln_b[0]), gm_ws[0], bst,
                     row(mix_out_gain[0]), w_out[0].astype(bf16))
    return _ffn_call(x1, mod3, row(norm2_gain[0]), w_up[0].astype(bf16), ffn_dw_w[0],
                     row(ffn_dw_b[0]), w_down[0].astype(bf16), row(final_gain))
```

```python
import jax
import jax.numpy as jnp
from jax import lax
from jax.experimental import pallas as pl
from jax.experimental.pallas import tpu as pltpu

D_MODEL = 1024
D_CONV = 512
D_GMLP = 512
HEAD_DIM = 64
N_GMLP_HEADS = 8
N_HEAD_PAIRS = N_GMLP_HEADS // 2
CONV_WIDTH = 31
CHUNK = 128
D_FF = 2816
FFN_CONV_WIDTH = 3
N_MOD = 6
RMS_EPS = 1e-6
LN_EPS = 1e-5

SUBLANES = 8
CONV_HALO = 32
FFN_HALO = SUBLANES
ADA_TN = 512
MIX_T = 512
FFN_T = 512
FFN_FC = 256
VMEM_LIMIT_BYTES = 56 * 1024 * 1024

_GELU_C = 0.7978845608028654


def _sigmoid(v):
    return 1.0 / (1.0 + jnp.exp(-v))


def _gelu_tanh(v):
    return 0.5 * v * (1.0 + jnp.tanh(_GELU_C * (v + 0.044715 * (v * v * v))))


def _rms_scale(v, width):
    ms = jnp.sum(v * v, axis=-1, keepdims=True) * (1.0 / width)
    return v * lax.rsqrt(ms + RMS_EPS)


def _layer_norm(v, g, b, width):
    mu = jnp.sum(v, axis=-1, keepdims=True) * (1.0 / width)
    vc = v - mu
    var = jnp.sum(vc * vc, axis=-1, keepdims=True) * (1.0 / width)
    return vc * lax.rsqrt(var + LN_EPS) * g + b


def _ada_kernel(ct_ref, w_ref, b_ref, o_ref):
    ct = ct_ref[...]
    ca = ct * _sigmoid(ct)
    w = w_ref[...]
    rows = [jnp.sum(w * ca[:, b:b + 1], axis=0, keepdims=True) for b in range(ct.shape[1])]
    o_ref[...] = jnp.concatenate(rows, axis=0) + b_ref[...]


def _ada_call(c, w_ada, b_ada):
    bsz, d = c.shape
    n = w_ada.shape[1]
    return pl.pallas_call(
        _ada_kernel,
        grid=(n // ADA_TN,),
        in_specs=[
            pl.BlockSpec((d, bsz), lambda j: (0, 0)),
            pl.BlockSpec((d, ADA_TN), lambda j: (0, j)),
            pl.BlockSpec((1, ADA_TN), lambda j: (0, j)),
        ],
        out_specs=pl.BlockSpec((bsz, ADA_TN), lambda j: (0, j)),
        out_shape=jax.ShapeDtypeStruct((bsz, n), jnp.float32),
        compiler_params=pltpu.CompilerParams(dimension_semantics=("arbitrary",)),
        name="ada",
    )(c.T, w_ada, b_ada.reshape(1, n))


def _mixer_kernel(x_ref, mod_ref, g1_ref, win_ref, cw_ref, cb_ref, clg_ref, clb_ref,
                  glg_ref, glb_ref, ws_ref, bst_ref, og_ref, wout_ref,
                  o_ref, a_scr, wsp_scr):
    b = pl.program_id(0)
    s = pl.program_id(1)
    t = x_ref.shape[1]
    n_chunks = t // CHUNK

    @pl.when(jnp.logical_and(b == 0, s == 0))
    def _():
        row = lax.broadcasted_iota(jnp.int32, (CHUNK, CHUNK), 0)
        col = lax.broadcasted_iota(jnp.int32, (CHUNK, CHUNK), 1)
        keep = col <= row
        for h in range(N_GMLP_HEADS):
            w = jnp.where(keep, ws_ref[h], 0.0).astype(jnp.bfloat16)
            wsp_scr[h // 2, :, (h % 2) * CHUNK:(h % 2 + 1) * CHUNK] = w

    @pl.when(s == 0)
    def _():
        a_scr[...] = jnp.zeros((CONV_HALO, D_CONV), jnp.float32)

    x = x_ref[0]
    sh1 = mod_ref[0, :, 0:D_MODEL]
    sc1 = mod_ref[0, :, D_MODEL:2 * D_MODEL]
    gt1 = mod_ref[0, :, 2 * D_MODEL:3 * D_MODEL]
    h = _rms_scale(x, D_MODEL) * (g1_ref[...] * (1.0 + sc1)) + sh1
    hb = h.astype(jnp.bfloat16)

    def proj(lo, width):
        return jnp.dot(hb, win_ref[:, lo:lo + width], preferred_element_type=jnp.float32)

    a = proj(0, D_CONV) * _sigmoid(proj(D_CONV, D_CONV))
    a_ext = jnp.concatenate([a_scr[...], a], axis=0)
    a_scr[...] = a[t - CONV_HALO:t, :]
    n_ext = CONV_HALO + t
    shifted = [a_ext] + [pltpu.roll(a_ext, n_ext - r, axis=0) for r in range(1, SUBLANES)]
    base = CONV_HALO - (CONV_WIDTH - 1)
    acc = cb_ref[...]
    for k in range(CONV_WIDTH):
        q, r = divmod(base + k, SUBLANES)
        acc = acc + cw_ref[k:k + 1, :] * shifted[r][SUBLANES * q:SUBLANES * q + t, :]
    av = _layer_norm(acc, clg_ref[...], clb_ref[...], D_CONV)
    av = av * _sigmoid(av)
    ya = _rms_scale(av, D_CONV) * og_ref[:, 0:D_CONV]

    gu = _gelu_tanh(proj(2 * D_CONV, D_GMLP))
    gv = _layer_norm(_gelu_tanh(proj(2 * D_CONV + D_GMLP, D_GMLP)),
                     glg_ref[...], glb_ref[...], D_GMLP).astype(jnp.bfloat16)
    lane = lax.broadcasted_iota(jnp.int32, (CHUNK, 2 * HEAD_DIM), 1)
    low = lane < HEAD_DIM
    zero = jnp.zeros((CHUNK, 2 * HEAD_DIM), jnp.bfloat16)
    sp_rows = []
    for j in range(n_chunks):
        sp_cols = []
        for p in range(N_HEAD_PAIRS):
            gvp = gv[j * CHUNK:(j + 1) * CHUNK, p * 2 * HEAD_DIM:(p + 1) * 2 * HEAD_DIM]
            rhs = jnp.concatenate([jnp.where(low, gvp, zero), jnp.where(low, zero, gvp)], axis=0)
            sp_cols.append(jnp.dot(wsp_scr[p], rhs, preferred_element_type=jnp.float32))
        sp_rows.append(jnp.concatenate(sp_cols, axis=1) + bst_ref[...])
    g = gu * jnp.concatenate(sp_rows, axis=0)
    yg = _rms_scale(g, D_GMLP) * og_ref[:, D_CONV:D_CONV + D_GMLP]

    out = (jnp.dot(ya.astype(jnp.bfloat16), wout_ref[0:D_CONV, :], preferred_element_type=jnp.float32)
           + jnp.dot(yg.astype(jnp.bfloat16), wout_ref[D_CONV:D_CONV + D_GMLP, :],
                     preferred_element_type=jnp.float32))
    o_ref[0] = x + gt1 * out


def _const_spec(shape):
    zeros = (0,) * len(shape)
    return pl.BlockSpec(shape, lambda b, s: zeros, pipeline_mode=pl.Buffered(1))


def _mixer_call(x, mod3, g1, w_in, cw, cb, clg, clb, glg, glb, ws, bst, og, w_out):
    bsz, seq, d = x.shape
    t = MIX_T
    tile = pl.BlockSpec((1, t, d), lambda b, s: (b, s, 0))
    return pl.pallas_call(
        _mixer_kernel,
        grid=(bsz, seq // t),
        in_specs=[
            tile,
            pl.BlockSpec((1, 1, N_MOD * d), lambda b, s: (b, 0, 0)),
            _const_spec((1, d)),
            _const_spec((d, 2 * d)),
            _const_spec((CONV_WIDTH, D_CONV)),
            _const_spec((1, D_CONV)),
            _const_spec((1, D_CONV)),
            _const_spec((1, D_CONV)),
            _const_spec((1, D_GMLP)),
            _const_spec((1, D_GMLP)),
            _const_spec((N_GMLP_HEADS, CHUNK, CHUNK)),
            _const_spec((CHUNK, D_GMLP)),
            _const_spec((1, d)),
            _const_spec((d, d)),
        ],
        out_specs=tile,
        out_shape=jax.ShapeDtypeStruct(x.shape, jnp.float32),
        scratch_shapes=[
            pltpu.VMEM((CONV_HALO, D_CONV), jnp.float32),
            pltpu.VMEM((N_HEAD_PAIRS, CHUNK, 2 * CHUNK), jnp.bfloat16),
        ],
        compiler_params=pltpu.CompilerParams(
            dimension_semantics=("arbitrary", "arbitrary"),
            vmem_limit_bytes=VMEM_LIMIT_BYTES),
        name="mixer",
    )(x, mod3, g1, w_in, cw, cb, clg, clb, glg, glb, ws, bst, og, w_out)


def _ffn_kernel(x_ref, mod_ref, g2_ref, wup_ref, dw_ref, db_ref, wdn_ref, gf_ref,
                o_ref, hist_scr, act_scr):
    s = pl.program_id(1)
    t = x_ref.shape[1]
    n_fc = D_FF // FFN_FC

    @pl.when(s == 0)
    def _():
        hist_scr[...] = jnp.zeros(hist_scr.shape, jnp.float32)

    x = x_ref[0]
    sh2 = mod_ref[0, :, 3 * D_MODEL:4 * D_MODEL]
    sc2 = mod_ref[0, :, 4 * D_MODEL:5 * D_MODEL]
    gt2 = mod_ref[0, :, 5 * D_MODEL:6 * D_MODEL]
    h = _rms_scale(x, D_MODEL) * (g2_ref[...] * (1.0 + sc2)) + sh2
    hb = h.astype(jnp.bfloat16)

    def conv_cols(lo):
        u = jnp.dot(hb, wup_ref[:, lo:lo + FFN_FC], preferred_element_type=jnp.float32)
        ext = jnp.concatenate([hist_scr[:, lo:lo + FFN_FC], u], axis=0)
        hist_scr[:, lo:lo + FFN_FC] = u[t - FFN_HALO:t, :]
        acc = db_ref[:, lo:lo + FFN_FC] + dw_ref[FFN_CONV_WIDTH - 1:FFN_CONV_WIDTH, lo:lo + FFN_FC] * u
        for j in range(1, FFN_CONV_WIDTH):
            k = FFN_CONV_WIDTH - 1 - j
            back = pltpu.roll(ext, j, axis=0)[FFN_HALO:FFN_HALO + t, :]
            acc = acc + dw_ref[k:k + 1, lo:lo + FFN_FC] * back
        return acc

    for c in range(n_fc):
        val = conv_cols(c * FFN_FC)
        gate = conv_cols(D_FF + c * FFN_FC)
        act_scr[:, c * FFN_FC:(c + 1) * FFN_FC] = (gate * _sigmoid(gate) * val).astype(jnp.bfloat16)

    y = jnp.dot(act_scr[...], wdn_ref[...], preferred_element_type=jnp.float32)
    x2 = x + gt2 * y
    o_ref[0] = _rms_scale(x2, D_MODEL) * gf_ref[...]


def _ffn_call(x1, mod3, g2, w_up, dw, db, w_down, gf):
    bsz, seq, d = x1.shape
    t = FFN_T
    tile = pl.BlockSpec((1, t, d), lambda b, s: (b, s, 0))
    return pl.pallas_call(
        _ffn_kernel,
        grid=(bsz, seq // t),
        in_specs=[
            tile,
            pl.BlockSpec((1, 1, N_MOD * d), lambda b, s: (b, 0, 0)),
            _const_spec((1, d)),
            _const_spec((d, 2 * D_FF)),
            _const_spec((FFN_CONV_WIDTH, 2 * D_FF)),
            _const_spec((1, 2 * D_FF)),
            _const_spec((D_FF, d)),
            _const_spec((1, d)),
        ],
        out_specs=tile,
        out_shape=jax.ShapeDtypeStruct(x1.shape, jnp.float32),
        scratch_shapes=[
            pltpu.VMEM((FFN_HALO, 2 * D_FF), jnp.float32),
            pltpu.VMEM((t, D_FF), jnp.bfloat16),
        ],
        compiler_params=pltpu.CompilerParams(
            dimension_semantics=("arbitrary", "arbitrary"),
            vmem_limit_bytes=VMEM_LIMIT_BYTES),
        name="ffn",
    )(x1, mod3, g2, w_up, dw, db, w_down, gf)


def kernel(x, c, w_ada, b_ada, norm1_gain, w_in, conv_dw_w, conv_dw_b, conv_ln_g, conv_ln_b,
           gm_ln_g, gm_ln_b, gm_ws, gm_bs, mix_out_gain, w_out, norm2_gain, w_up,
           ffn_dw_w, ffn_dw_b, w_down, final_gain):
    bsz, seq, d = x.shape
    assert (d, seq % MIX_T, seq % FFN_T, MIX_T % CHUNK) == (D_MODEL, 0, 0, 0)
    assert w_ada.shape[0] == 1, "single-layer block"
    bf16 = jnp.bfloat16
    row = lambda v: v.reshape(1, -1)

    mod = _ada_call(c, w_ada[0], b_ada[0])
    mod3 = mod.reshape(bsz, 1, N_MOD * d)
    bst = jnp.repeat(gm_bs[0].T, HEAD_DIM, axis=1)

    x1 = _mixer_call(x, mod3, row(norm1_gain[0]), w_in[0].astype(bf16), conv_dw_w[0],
                     row(conv_dw_b[0]), row(conv_ln_g[0]), row(conv_ln_b[0]),
                     row(gm_ln_g[0]), row(gm_ln_b[0]), gm_ws[0], bst,
                     row(mix_out_gain[0]), w_out[0].astype(bf16))
    return _ffn_call(x1, mod3, row(norm2_gain[0]), w_up[0].astype(bf16), ffn_dw_w[0],
                     row(ffn_dw_b[0]), w_down[0].astype(bf16), row(final_gain))
```
